```python
import jax
import jax.numpy as jnp
from jax import lax
import numpy as np

D_MODEL = 2048
BATCH = 1
SEQ = 8192
DEPTH = 4

HEAD_DIM = 128
D_FF = 5504
NORM_EPS = 1e-6
BLK = 128
ROPE_THETA = 500000.0
ROPE_DIM = HEAD_DIM // 4

RET_HEADS = 8
RET_DK = 128
RET_DV = 256
RET_CHUNK = 128
RET_ROPE_BASE = 10000.0

DIL_HEADS = 8
DIL_PATTERNS = ((128, 1), (512, 4), (2048, 16))

NSA_HEADS = 16
NSA_KV_HEADS = 4
CMP_LEN = 32
CMP_STRIDE = 16
CMP_HIDDEN = 256
SEL_LEN = 64
SEL_TOPK = 16
WIN_LEN = 512
FORCE_BONUS = 1e4

AB_SIZES = (RET_HEADS * RET_DK, RET_HEADS * RET_DK, RET_HEADS * RET_DV, RET_HEADS * RET_DV,
            DIL_HEADS * HEAD_DIM, DIL_HEADS * HEAD_DIM, DIL_HEADS * HEAD_DIM)
AB_IN = sum(AB_SIZES)
AB_OUT = RET_HEADS * RET_DV + DIL_HEADS * HEAD_DIM
C_SIZES = (NSA_HEADS * HEAD_DIM,) + (NSA_KV_HEADS * HEAD_DIM,) * 6 + (3 * NSA_HEADS,)
C_IN = sum(C_SIZES)
C_OUT = NSA_HEADS * HEAD_DIM

kernel_name = "hybrid_retention_dilated_nsa_macaron"


def rmsnorm(x, g):
    xf = x.astype(jnp.float32)
    y = xf * lax.rsqrt(jnp.mean(xf * xf, axis=-1, keepdims=True) + NORM_EPS)
    return (y * g.astype(jnp.float32)).astype(x.dtype)


def swiglu(x, w1, w3, w2):
    return (jax.nn.silu(x @ w1) * (x @ w3)) @ w2


def rope(x, positions, rot_dim, base):
    half = rot_dim // 2
    inv = base ** (-jnp.arange(half, dtype=jnp.float32) / half)
    ang = positions.astype(jnp.float32)[:, :, None, None] * inv
    cos, sin = jnp.cos(ang), jnp.sin(ang)
    xf = x[..., :rot_dim].astype(jnp.float32)
    x1, x2 = xf[..., :half], xf[..., half:]
    rot = jnp.concatenate([x1 * cos - x2 * sin, x2 * cos + x1 * sin], axis=-1).astype(x.dtype)
    return jnp.concatenate([rot, x[..., rot_dim:]], axis=-1)


def split_cols(p, sizes):
    offs = np.cumsum(sizes)[:-1].tolist()
    return jnp.split(p, offs, axis=-1)


def masked_softmax(s, mask):
    s = jnp.where(mask, s, -jnp.inf)
    m = jnp.max(s, axis=-1, keepdims=True)
    m = jnp.where(jnp.isfinite(m), m, 0.0)
    p = jnp.exp(s - m)
    return p / jnp.maximum(jnp.sum(p, axis=-1, keepdims=True), 1e-30)


def retention(q, k, v, g):
    b_, s_, n_h, dk = q.shape
    dv = v.shape[-1]
    c = RET_CHUNK
    n_c = s_ // c
    log_gamma = jnp.log1p(-jnp.power(2.0, -5.0 - jnp.arange(n_h, dtype=jnp.float32)))
    qc = q.reshape(b_, n_c, c, n_h, dk).astype(jnp.float32)
    kc = (k.astype(jnp.float32) * dk ** -0.5).reshape(b_, n_c, c, n_h, dk)
    vc = v.reshape(b_, n_c, c, n_h, dv).astype(jnp.float32)
    idx = jnp.arange(c, dtype=jnp.float32)
    rel = idx[:, None] - idx[None, :]
    decay = jnp.where(rel[None] >= 0.0,
                      jnp.exp(log_gamma[:, None, None] * jnp.maximum(rel, 0.0)[None]), 0.0)
    scores = jnp.einsum('bnqhd,bnkhd->bnhqk', qc, kc) * decay[None, None]
    o_inner = jnp.einsum('bnhqk,bnkhe->bnqhe', scores, vc)
    zeta = jnp.exp(log_gamma[:, None] * (c - 1.0 - idx)[None])
    kv = jnp.einsum('bnkhd,hk,bnkhe->bnhde', kc, zeta, vc)
    chunk_decay = jnp.exp(log_gamma * c)[None, :, None, None]

    def step(state, kv_i):
        return state * chunk_decay + kv_i, state

    init = jnp.zeros((b_, n_h, dk, dv), jnp.float32)
    _, prev = lax.scan(step, init, jnp.moveaxis(kv, 1, 0))
    prev = jnp.moveaxis(prev, 0, 1)
    xi = jnp.exp(log_gamma[:, None] * (idx + 1.0)[None])
    o_cross = jnp.einsum('bnqhd,bnhde,hq->bnqhe', qc, prev, xi)
    o = (o_inner + o_cross).reshape(b_, s_, n_h, dv)
    mu = jnp.mean(o, axis=-1, keepdims=True)
    var = jnp.mean(jnp.square(o - mu), axis=-1, keepdims=True)
    o = (o - mu) * lax.rsqrt(var + NORM_EPS)
    return (jax.nn.silu(g.astype(jnp.float32)) * o).astype(q.dtype)


def dilated_branch(q, k, v, dil, span, scale):
    b_, s_, n_h, dh = q.shape
    unit = dil * BLK
    s_pad = -(-s_ // unit) * unit
    length = s_pad // dil
    nb = length // BLK

    def sub(x):
        x = jnp.pad(x, ((0, 0), (0, s_pad - s_), (0, 0), (0, 0))).reshape(b_, length, dil, n_h, dh)
        return jnp.moveaxis(x, 2, 1).reshape(b_, dil, nb, BLK, n_h, dh)

    def with_prev(x):
        prev = jnp.pad(x, ((0, 0), (0, 0), (1, 0), (0, 0), (0, 0), (0, 0)))[:, :, :-1]
        return jnp.concatenate([prev, x], axis=3)

    qs = sub(q)
    kk = with_prev(sub(k))
    vv = with_prev(sub(v))
    s = jnp.einsum('brnqhd,brnkhd->brnhqk', qs, kk, preferred_element_type=jnp.float32) * scale
    qi = jnp.arange(BLK)[:, None] + BLK
    kj = jnp.arange(2 * BLK)[None, :]
    dist = qi - kj
    kglob = jnp.arange(nb)[:, None, None] * BLK + kj[None] - BLK
    mask = (dist >= 0)[None] & (dist <= span)[None] & (kglob >= 0)
    s = jnp.where(mask[None, None, :, None], s, -jnp.inf)
    m = jnp.max(s, axis=-1, keepdims=True)
    p = jnp.exp(s - m)
    den = jnp.sum(p, axis=-1, keepdims=True)
    o = jnp.einsum('brnhqk,brnkhd->brnqhd', p / den, vv.astype(jnp.float32))
    lse = jnp.swapaxes((m + jnp.log(den))[..., 0], -1, -2)
    o = jnp.moveaxis(o.reshape(b_, dil, length, n_h, dh), 1, 2).reshape(b_, s_pad, n_h, dh)[:, :s_]
    lse = jnp.moveaxis(lse.reshape(b_, dil, length, n_h), 1, 2).reshape(b_, s_pad, n_h)[:, :s_]
    return o, lse


def dilated_attention(q, k, v):
    scale = q.shape[-1] ** -0.5
    outs, lses = [], []
    for window, dil in DIL_PATTERNS:
        o, lse = dilated_branch(q, k, v, dil, window // dil, scale)
        outs.append(o)
        lses.append(lse)
    w = jax.nn.softmax(jnp.stack(lses, axis=0), axis=0)
    return jnp.sum(w[..., None] * jnp.stack(outs, axis=0), axis=0).astype(q.dtype)


def retention_dilated_mixer(h, positions, w_in, w_out):
    b_, s_, _ = h.shape
    qa, ka, va, ga, qb, kb, vb = split_cols(h @ w_in, AB_SIZES)
    qa = rope(qa.reshape(b_, s_, RET_HEADS, RET_DK), positions, RET_DK, RET_ROPE_BASE)
    ka = rope(ka.reshape(b_, s_, RET_HEADS, RET_DK), positions, RET_DK, RET_ROPE_BASE)
    oa = retention(qa, ka, va.reshape(b_, s_, RET_HEADS, RET_DV), ga.reshape(b_, s_, RET_HEADS, RET_DV))
    qb = rope(qb.reshape(b_, s_, DIL_HEADS, HEAD_DIM), positions, ROPE_DIM, ROPE_THETA)
    kb = rope(kb.reshape(b_, s_, DIL_HEADS, HEAD_DIM), positions, ROPE_DIM, ROPE_THETA)
    ob = dilated_attention(qb, kb, vb.reshape(b_, s_, DIL_HEADS, HEAD_DIM))
    cat = jnp.concatenate([oa.reshape(b_, s_, RET_HEADS * RET_DV),
                           ob.reshape(b_, s_, DIL_HEADS * HEAD_DIM)], axis=-1)
    return cat @ w_out


def nsa_attention(q, kc, vc, ks, vs, kw, vw, gates, ck_w1, ck_w2, cv_w1, cv_w2, pe_k, pe_v):
    b_, s_, n_h, dh = q.shape
    n_g = kc.shape[2]
    hpg = n_h // n_g
    scale = dh ** -0.5
    n_cmp = (s_ - CMP_LEN) // CMP_STRIDE + 1
    starts = jnp.arange(n_cmp) * CMP_STRIDE
    cmp_idx = starts[:, None] + jnp.arange(CMP_LEN)[None]
    cmp_end = starts + CMP_LEN - 1

    def compress(x, pe, w1, w2):
        blocks = x[:, cmp_idx] + pe[:, None, :]
        blocks = jnp.moveaxis(blocks, 3, 2).reshape(b_, n_cmp, n_g, CMP_LEN * dh)
        return jax.nn.gelu(blocks @ w1) @ w2

    ck = compress(kc, pe_k, ck_w1, ck_w2)
    cv = compress(vc, pe_v, cv_w1, cv_w2)
    n_sel = s_ // SEL_LEN
    n_top = min(SEL_TOPK, n_sel)
    sel_ids = jnp.arange(n_sel)
    sel_start = sel_ids * SEL_LEN
    overlap = ((starts[:, None] <= sel_start[None] + SEL_LEN - 1)
               & (cmp_end[:, None] >= sel_start[None])).astype(jnp.float32)
    kb = jnp.moveaxis(ks.reshape(b_, n_sel, SEL_LEN, n_g, dh), 3, 1)
    vb = jnp.moveaxis(vs.reshape(b_, n_sel, SEL_LEN, n_g, dh), 3, 1)
    kw_pad = jnp.pad(kw, ((0, 0), (WIN_LEN, 0), (0, 0), (0, 0)))
    vw_pad = jnp.pad(vw, ((0, 0), (WIN_LEN, 0), (0, 0), (0, 0)))
    b_idx = jnp.arange(b_)[:, None, None, None]
    g_idx = jnp.arange(n_g)[None, :, None, None]
    n_qb = s_ // BLK
    q_blocks = jnp.moveaxis(q.reshape(b_, n_qb, BLK, n_g, hpg, dh), 1, 0)
    g_blocks = jnp.moveaxis(gates.reshape(b_, n_qb, BLK, n_g, hpg, 3), 1, 0)

    def block(args):
        bi, qblk, gblk = args
        t = bi * BLK + jnp.arange(BLK)
        s = jnp.einsum('bqgjd,bngd->bgjqn', qblk, ck, preferred_element_type=jnp.float32) * scale
        p_cmp = masked_softmax(s, cmp_end[None, :] <= t[:, None])
        o_cmp = jnp.einsum('bgjqn,bngd->bqgjd', p_cmp, cv.astype(jnp.float32))
        imp = jnp.einsum('bgjqn,nm->bgqm', p_cmp, overlap)
        cur = t // SEL_LEN
        valid = sel_start[None, :] <= t[:, None]
        forced = (sel_ids[None] == 0) | (sel_ids[None] == cur[:, None]) | (sel_ids[None] == cur[:, None] - 1)
        score = jnp.where(valid, imp + jnp.where(forced, FORCE_BONUS, 0.0), -jnp.inf)
        top_val, top_idx = lax.top_k(score, n_top)
        ksel = kb[b_idx, g_idx, top_idx]
        vsel = vb[b_idx, g_idx, top_idx]
        pos = top_idx[..., None] * SEL_LEN + jnp.arange(SEL_LEN)
        smask = jnp.isfinite(top_val)[..., None] & (pos <= t[None, None, :, None, None])
        s = jnp.einsum('bqgjd,bgqnrd->bgjqnr', qblk, ksel, preferred_element_type=jnp.float32) * scale
        flat = (b_, n_g, hpg, BLK, n_top * SEL_LEN)
        p = masked_softmax(s.reshape(flat),
                           jnp.broadcast_to(smask[:, :, None], s.shape).reshape(flat))
        p = p.reshape(s.shape)
        o_slc = jnp.einsum('bgjqnr,bgqnrd->bqgjd', p, vsel.astype(jnp.float32))
        kwin = lax.dynamic_slice_in_dim(kw_pad, bi * BLK, BLK + WIN_LEN, axis=1)
        vwin = lax.dynamic_slice_in_dim(vw_pad, bi * BLK, BLK + WIN_LEN, axis=1)
        kpos = bi * BLK - WIN_LEN + jnp.arange(BLK + WIN_LEN)
        diff = t[:, None] - kpos[None]
        wmask = (diff >= 0) & (diff < WIN_LEN) & (kpos[None] >= 0)
        s = jnp.einsum('bqgjd,bkgd->bgjqk', qblk, kwin, preferred_element_type=jnp.float32) * scale
        o_win = jnp.einsum('bgjqk,bkgd->bqgjd', masked_softmax(s, wmask), vwin.astype(jnp.float32))
        gs = jax.nn.sigmoid(gblk.astype(jnp.float32))
        o = gs[..., 0:1] * o_cmp + gs[..., 1:2] * o_slc + gs[..., 2:3] * o_win
        return o.astype(q.dtype)

    out = lax.map(block, (jnp.arange(n_qb), q_blocks, g_blocks))
    return jnp.moveaxis(out, 0, 1).reshape(b_, s_, n_h * dh)


def nsa_mixer(h, positions, w_in, w_out, ck_w1, ck_w2, cv_w1, cv_w2, pe_k, pe_v):
    b_, s_, _ = h.shape
    q, kc, vc, ks, vs, kw, vw, gates = split_cols(h @ w_in, C_SIZES)
    q = rope(q.reshape(b_, s_, NSA_HEADS, HEAD_DIM), positions, ROPE_DIM, ROPE_THETA)
    kv_shape = (b_, s_, NSA_KV_HEADS, HEAD_DIM)
    kc = rope(kc.reshape(kv_shape), positions, ROPE_DIM, ROPE_THETA)
    ks = rope(ks.reshape(kv_shape), positions, ROPE_DIM, ROPE_THETA)
    kw = rope(kw.reshape(kv_shape), positions, ROPE_DIM, ROPE_THETA)
    o = nsa_attention(q, kc, vc.reshape(kv_shape), ks, vs.reshape(kv_shape), kw, vw.reshape(kv_shape),
                      gates, ck_w1, ck_w2, cv_w1, cv_w2, pe_k, pe_v)
    return o @ w_out


def setup_inputs(seed: int = 0) -> dict:
    key = jax.random.key(seed)
    ks = jax.random.split(key, 18)
    n_even = (DEPTH + 1) // 2
    n_odd = DEPTH // 2

    def w(k, shape, fan_in):
        return jax.random.normal(k, shape, jnp.float32) * fan_in ** -0.5

    x = jax.random.normal(ks[0], (BATCH, SEQ, D_MODEL), jnp.float32)
    positions = jnp.broadcast_to(jnp.arange(SEQ, dtype=jnp.int32)[None], (BATCH, SEQ))
    norm_g = 1.0 + 0.02 * jax.random.normal(ks[1], (DEPTH, 6, D_MODEL), jnp.float32)
    ffn_w1 = w(ks[2], (DEPTH, 2, D_MODEL, D_FF), D_MODEL)
    ffn_w3 = w(ks[3], (DEPTH, 2, D_MODEL, D_FF), D_MODEL)
    ffn_w2 = w(ks[4], (DEPTH, 2, D_FF, D_MODEL), D_FF)
    ab_w_in = w(ks[5], (n_even, D_MODEL, AB_IN), D_MODEL)
    ab_w_out = w(ks[6], (n_even, AB_OUT, D_MODEL), AB_OUT)
    c_w_in = w(ks[7], (n_odd, D_MODEL, C_IN), D_MODEL)
    c_w_out = w(ks[8], (n_odd, C_OUT, D_MODEL), C_OUT)
    c_ck_w1 = w(ks[9], (n_odd, CMP_LEN * HEAD_DIM, CMP_HIDDEN), CMP_LEN * HEAD_DIM)
    c_ck_w2 = w(ks[10], (n_odd, CMP_HIDDEN, HEAD_DIM), CMP_HIDDEN)
    c_cv_w1 = w(ks[11], (n_odd, CMP_LEN * HEAD_DIM, CMP_HIDDEN), CMP_LEN * HEAD_DIM)
    c_cv_w2 = w(ks[12], (n_odd, CMP_HIDDEN, HEAD_DIM), CMP_HIDDEN)
    c_pe_k = 0.02 * jax.random.normal(ks[13], (n_odd, CMP_LEN, HEAD_DIM), jnp.float32)
    c_pe_v = 0.02 * jax.random.normal(ks[14], (n_odd, CMP_LEN, HEAD_DIM), jnp.float32)
    return {"x": x, "positions": positions, "norm_g": norm_g, "ffn_w1": ffn_w1, "ffn_w3": ffn_w3,
            "ffn_w2": ffn_w2, "ab_w_in": ab_w_in, "ab_w_out": ab_w_out, "c_w_in": c_w_in,
            "c_w_out": c_w_out, "c_ck_w1": c_ck_w1, "c_ck_w2": c_ck_w2, "c_cv_w1": c_cv_w1,
            "c_cv_w2": c_cv_w2, "c_pe_k": c_pe_k, "c_pe_v": c_pe_v}


def reference(x, positions, norm_g, ffn_w1, ffn_w3, ffn_w2, ab_w_in, ab_w_out, c_w_in, c_w_out,
              c_ck_w1, c_ck_w2, c_cv_w1, c_cv_w2, c_pe_k, c_pe_v):
    for layer in range(DEPTH):
        g = norm_g[layer]
        h = rmsnorm(x, g[0])
        x = x + 0.5 * rmsnorm(swiglu(h, ffn_w1[layer, 0], ffn_w3[layer, 0], ffn_w2[layer, 0]), g[1])
        h = rmsnorm(x, g[2])
        j = layer // 2
        if layer % 2 == 0:
            m = retention_dilated_mixer(h, positions, ab_w_in[j], ab_w_out[j])
        else:
            m = nsa_mixer(h, positions, c_w_in[j], c_w_out[j], c_ck_w1[j], c_ck_w2[j],
                          c_cv_w1[j], c_cv_w2[j], c_pe_k[j], c_pe_v[j])
        x = x + rmsnorm(m, g[3])
        h = rmsnorm(x, g[4])
        x = x + 0.5 * rmsnorm(swiglu(h, ffn_w1[layer, 1], ffn_w3[layer, 1], ffn_w2[layer, 1]), g[5])
    return x
```

```python
import functools

import jax
import jax.numpy as jnp
from jax import lax
from jax.experimental import pallas as pl
from jax.experimental.pallas import tpu as pltpu

D_MODEL = 2048
DEPTH = 4
HEAD_DIM = 128
D_FF = 5504
NORM_EPS = 1e-6
BLK = 128
ROPE_THETA = 500000.0
ROPE_DIM = HEAD_DIM // 4

RET_HEADS = 8
RET_DK = 128
RET_DV = 256
RET_CHUNK = 128
RET_ROPE_BASE = 10000.0

DIL_HEADS = 8
DIL_PATTERNS = ((128, 1), (512, 4), (2048, 16))
DIL_REACH = 2048
DIL_KEYS = DIL_REACH + BLK

NSA_HEADS = 16
NSA_KV_HEADS = 4
NSA_HPG = NSA_HEADS // NSA_KV_HEADS
CMP_LEN = 32
CMP_STRIDE = 16
CMP_HIDDEN = 256
SEL_LEN = 64
SEL_TOPK = 16
WIN_LEN = 512
WIN_KEYS = WIN_LEN + BLK
FORCE_BONUS = 1e4

AB_A_COLS = 2 * RET_HEADS * RET_DK + 2 * RET_HEADS * RET_DV
AB_B_COLS = 3 * DIL_HEADS * HEAD_DIM
C_MAIN_COLS = NSA_HEADS * HEAD_DIM + 6 * NSA_KV_HEADS * HEAD_DIM
C_GATES = 3 * NSA_HEADS

LANES = 128
V7X_VMEM_BYTES = 64 * 1024 * 1024

FF_MAIN = (D_FF // 768) * 768
FF_TAIL = D_FF - FF_MAIN
assert FF_TAIL == LANES and FF_MAIN % LANES == 0

NEG = -1e30
F32 = jnp.float32
BF16 = jnp.bfloat16


def _params(n_axes, vmem_mb):
    return pltpu.CompilerParams(
        dimension_semantics=("arbitrary",) * n_axes,
        vmem_limit_bytes=min(vmem_mb * 1024 * 1024, V7X_VMEM_BYTES - 4 * 1024 * 1024),
    )


def _dot(a, b):
    return jnp.dot(a, b, preferred_element_type=F32)


def _dot_nt(a, b):
    return lax.dot_general(a, b, (((1,), (1,)), ((), ())), preferred_element_type=F32)


def _sigmoid(x):
    return 1.0 / (1.0 + jnp.exp(-x))


def _rms(y, g):
    return y * lax.rsqrt(jnp.mean(y * y, axis=-1, keepdims=True) + NORM_EPS) * g


def _prenorm_kernel(x_ref, g_ref, o_ref, *, gi):
    o_ref[...] = _rms(x_ref[...], g_ref[gi:gi + 1, :]).astype(o_ref.dtype)


def prenorm(x, norm_g, layer, gi, tm=512):
    s, d = x.shape
    return pl.pallas_call(
        functools.partial(_prenorm_kernel, gi=gi),
        grid=(s // tm,),
        in_specs=[pl.BlockSpec((tm, d), lambda i: (i, 0)),
                  pl.BlockSpec((None, 6, d), lambda i: (layer, 0, 0))],
        out_specs=pl.BlockSpec((tm, d), lambda i: (i, 0)),
        out_shape=jax.ShapeDtypeStruct((s, d), BF16),
        compiler_params=_params(1, 24),
        name="prenorm",
    )(x, norm_g)


def _up_kernel(h_ref, w1_ref, w3_ref, o_ref, w1b, w3b):
    @pl.when(pl.program_id(1) == 0)
    def _():
        w1b[...] = w1_ref[...].astype(BF16)
        w3b[...] = w3_ref[...].astype(BF16)

    h = h_ref[...]
    a = _dot(h, w1b[...])
    b = _dot(h, w3b[...])
    o_ref[...] = (a * _sigmoid(a) * b).astype(o_ref.dtype)


def ffn_up(h, w1, w3, layer, which, col_block0, n_cols, tn, tm=512):
    s, d = h.shape
    wspec = pl.BlockSpec((None, None, d, tn), lambda j, i: (layer, which, 0, col_block0 + j))
    return pl.pallas_call(
        _up_kernel,
        grid=(n_cols // tn, s // tm),
        in_specs=[pl.BlockSpec((tm, d), lambda j, i: (i, 0)), wspec, wspec],
        out_specs=pl.BlockSpec((tm, tn), lambda j, i: (i, j)),
        out_shape=jax.ShapeDtypeStruct((s, n_cols), BF16),
        scratch_shapes=[pltpu.VMEM((d, tn), BF16), pltpu.VMEM((d, tn), BF16)],
        compiler_params=_params(2, 52),
        name="ffn_up",
    )(h, w1, w3)


def _rope_full(p, cos_ref, sin_ref):
    return p * cos_ref[...] + pltpu.roll(p, 64, axis=1) * sin_ref[...]


def _rope_part(p, c_ref, s_up_ref, s_dn_ref):
    return (p * c_ref[...] + pltpu.roll(p, 16, axis=1) * s_up_ref[...]
            + pltpu.roll(p, LANES - 16, axis=1) * s_dn_ref[...])


def _proj_kernel(h_ref, w_ref, *rest, rope, rope_tiles, tn):
    tabs, (o_ref, wb) = rest[:-2], rest[-2:]
    j = pl.program_id(0)

    @pl.when(pl.program_id(1) == 0)
    def _():
        wb[...] = w_ref[...].astype(BF16)

    acc = _dot(h_ref[...], wb[...])
    if rope is None:
        o_ref[...] = acc.astype(o_ref.dtype)
        return
    use = functools.reduce(jnp.logical_or, [j == t for t in rope_tiles])
    for c in range(tn // LANES):
        p = acc[:, c * LANES:(c + 1) * LANES]
        r = _rope_full(p, *tabs) if rope == "full" else _rope_part(p, *tabs)
        o_ref[:, c * LANES:(c + 1) * LANES] = jnp.where(use, r, p).astype(o_ref.dtype)


def proj(h, w, widx, col_block0, n_cols, tn, rope=None, rope_tiles=(), tabs=(),
         out_dtype=BF16, tm=512):
    s, d = h.shape
    if w.ndim == 3:
        wspec = pl.BlockSpec((None, d, tn), lambda j, i: (widx, 0, col_block0 + j))
    else:
        wspec = pl.BlockSpec((d, tn), lambda j, i: (0, col_block0 + j))
    tspec = pl.BlockSpec((tm, LANES), lambda j, i: (i, 0))
    return pl.pallas_call(
        functools.partial(_proj_kernel, rope=rope, rope_tiles=rope_tiles, tn=tn),
        grid=(n_cols // tn, s // tm),
        in_specs=[pl.BlockSpec((tm, d), lambda j, i: (i, 0)), wspec] + [tspec] * len(tabs),
        out_specs=pl.BlockSpec((tm, tn), lambda j, i: (i, j)),
        out_shape=jax.ShapeDtypeStruct((s, n_cols), out_dtype),
        scratch_shapes=[pltpu.VMEM((d, tn), BF16)],
        compiler_params=_params(2, 48),
        name="proj",
    )(h, w, *tabs)


def _down_kernel(*refs, n_k, has_tail, scale, gi_post, gi_next):
    lhs_ref, w_ref = refs[0], refs[1]
    refs = refs[2:]
    if has_tail:
        lt_ref, wt_ref = refs[0], refs[1]
        refs = refs[2:]
    x_ref, g_ref, gn_ref, xo_ref, ho_ref, acc = refs
    k = pl.program_id(1)

    @pl.when(k == 0)
    def _():
        if has_tail:
            acc[...] = _dot(lt_ref[...], wt_ref[...])
        else:
            acc[...] = jnp.zeros_like(acc)

    acc[...] += _dot(lhs_ref[...], w_ref[...])

    @pl.when(k == n_k - 1)
    def _():
        xn = x_ref[...] + scale * _rms(acc[...], g_ref[gi_post:gi_post + 1, :])
        xo_ref[...] = xn
        ho_ref[...] = _rms(xn, gn_ref[gi_next:gi_next + 1, :]).astype(ho_ref.dtype)


def down(lhs, w, widx, tk, x, norm_g, layer, gi_post, next_layer, gi_next, scale,
         tail=None, tm=512):
    s, kdim = lhs.shape
    d = x.shape[1]
    n_k = kdim // tk
    nlead = len(widx)
    wspec = pl.BlockSpec((None,) * nlead + (tk, d), lambda i, k: tuple(widx) + (k, 0))
    in_specs = [pl.BlockSpec((tm, tk), lambda i, k: (i, k)), wspec]
    args = [lhs, w]
    if tail is not None:
        lt, row_block = tail
        tw = lt.shape[1]
        in_specs += [pl.BlockSpec((tm, tw), lambda i, k: (i, 0)),
                     pl.BlockSpec((None,) * nlead + (tw, d),
                                  lambda i, k: tuple(widx) + (row_block, 0))]
        args += [lt, w]
    in_specs += [pl.BlockSpec((tm, d), lambda i, k: (i, 0)),
                 pl.BlockSpec((None, 6, d), lambda i, k: (layer, 0, 0)),
                 pl.BlockSpec((None, 6, d), lambda i, k: (next_layer, 0, 0))]
    args += [x, norm_g, norm_g]
    return pl.pallas_call(
        functools.partial(_down_kernel, n_k=n_k, has_tail=tail is not None, scale=scale,
                          gi_post=gi_post, gi_next=gi_next),
        grid=(s // tm, n_k),
        in_specs=in_specs,
        out_specs=[pl.BlockSpec((tm, d), lambda i, k: (i, 0)),
                   pl.BlockSpec((tm, d), lambda i, k: (i, 0))],
        out_shape=[jax.ShapeDtypeStruct((s, d), F32), jax.ShapeDtypeStruct((s, d), BF16)],
        scratch_shapes=[pltpu.VMEM((tm, d), F32)],
        compiler_params=_params(2, 48),
        name="down",
    )(*args)


def _retention_kernel(pa_ref, dec_ref, zs_ref, xi_ref, cd_ref, o_ref, state):
    @pl.when(pl.program_id(0) == 0)
    def _():
        state[...] = jnp.zeros_like(state)

    qo, ko = 0, RET_HEADS * RET_DK
    vo = 2 * RET_HEADS * RET_DK
    go = vo + RET_HEADS * RET_DV
    for h in range(RET_HEADS):
        q = pa_ref[:, qo + h * RET_DK: qo + (h + 1) * RET_DK]
        k = pa_ref[:, ko + h * RET_DK: ko + (h + 1) * RET_DK]
        v = pa_ref[:, vo + h * RET_DV: vo + (h + 1) * RET_DV]
        g = pa_ref[:, go + h * RET_DV: go + (h + 1) * RET_DV].astype(F32)
        st = state[h]
        scores = _dot_nt(q, k) * dec_ref[h]
        o = _dot(scores.astype(BF16), v)
        o = o + _dot(q, st.astype(BF16)) * xi_ref[h]
        vz = (v.astype(F32) * zs_ref[h]).astype(BF16)
        kv = _dot(k.astype(F32).T.astype(BF16), vz)
        state[h] = st * cd_ref[h] + kv
        d = o - jnp.mean(o, axis=-1, keepdims=True)
        on = d * lax.rsqrt(jnp.mean(d * d, axis=-1, keepdims=True) + NORM_EPS)
        o_ref[:, h * RET_DV:(h + 1) * RET_DV] = (g * _sigmoid(g) * on).astype(o_ref.dtype)


def _retention_tables():
    c = RET_CHUNK
    log_gamma = jnp.log1p(-jnp.power(2.0, -5.0 - jnp.arange(RET_HEADS, dtype=F32)))
    idx = jnp.arange(c, dtype=F32)
    rel = idx[:, None] - idx[None, :]
    decay = jnp.where(rel[None] >= 0.0,
                      jnp.exp(log_gamma[:, None, None] * jnp.maximum(rel, 0.0)[None]), 0.0)
    kscale = RET_DK ** -0.5
    zeta = jnp.exp(log_gamma[:, None] * (c - 1.0 - idx)[None])
    xi = jnp.exp(log_gamma[:, None] * (idx + 1.0)[None])
    cdec = jnp.exp(log_gamma * c)
    wide = (RET_HEADS, c, RET_DV)
    return (decay * kscale,
            jnp.broadcast_to((zeta * kscale)[:, :, None], wide),
            jnp.broadcast_to(xi[:, :, None], wide),
            jnp.broadcast_to(cdec[:, None, None], (RET_HEADS, 1, RET_DV)))


def retention(pa):
    s = pa.shape[0]
    c = RET_CHUNK
    dec, zs, xi, cd = _retention_tables()
    full = lambda a: pl.BlockSpec(a.shape, lambda n: (0,) * a.ndim)
    return pl.pallas_call(
        _retention_kernel,
        grid=(s // c,),
        in_specs=[pl.BlockSpec((c, AB_A_COLS), lambda n: (n, 0)),
                  full(dec), full(zs), full(xi), full(cd)],
        out_specs=pl.BlockSpec((c, RET_HEADS * RET_DV), lambda n: (n, 0)),
        out_shape=jax.ShapeDtypeStruct((s, RET_HEADS * RET_DV), BF16),
        scratch_shapes=[pltpu.VMEM((RET_HEADS, RET_DK, RET_DV), F32)],
        compiler_params=_params(1, 32),
        name="retention",
    )(pa, dec, zs, xi, cd)


def _dilated_kernel(q_ref, k_ref, v_ref, c_ref, o_ref, *, scale):
    i = pl.program_id(1)
    back = jnp.minimum(i, DIL_REACH // BLK)
    start = pl.multiple_of((i - back) * BLK, BLK)
    u0 = pl.multiple_of((DIL_REACH // BLK - back) * BLK, BLK)
    kw = k_ref[pl.ds(start, DIL_KEYS), :]
    vw = v_ref[pl.ds(start, DIL_KEYS), :]
    mult = c_ref[:, pl.ds(u0, DIL_KEYS)]
    s = _dot_nt(q_ref[...], kw) * scale
    sm = jnp.where(mult > 0.0, s, NEG)
    m = jnp.max(sm, axis=-1, keepdims=True)
    p = mult * jnp.exp(sm - m)
    den = jnp.sum(p, axis=-1, keepdims=True)
    o_ref[...] = (_dot(p.astype(BF16), vw) / den).astype(o_ref.dtype)


def _dilated_multiplicity():
    a = jnp.arange(BLK)[:, None]
    u = jnp.arange(DIL_REACH + DIL_KEYS)[None, :]
    dist = DIL_REACH + a - u
    mult = jnp.zeros(dist.shape, F32)
    for window, dil in DIL_PATTERNS:
        span = window // dil
        mult = mult + ((dist >= 0) & (dist % dil == 0) & (dist <= span * dil)).astype(F32)
    return mult


def dilated(pb):
    s = pb.shape[0]
    assert s >= DIL_KEYS and s % BLK == 0
    mult = _dilated_multiplicity()
    nh = DIL_HEADS
    return pl.pallas_call(
        functools.partial(_dilated_kernel, scale=HEAD_DIM ** -0.5),
        grid=(nh, s // BLK),
        in_specs=[pl.BlockSpec((BLK, HEAD_DIM), lambda h, i: (i, h)),
                  pl.BlockSpec((s, HEAD_DIM), lambda h, i: (0, nh + h)),
                  pl.BlockSpec((s, HEAD_DIM), lambda h, i: (0, 2 * nh + h)),
                  pl.BlockSpec(mult.shape, lambda h, i: (0, 0))],
        out_specs=pl.BlockSpec((BLK, HEAD_DIM), lambda h, i: (i, h)),
        out_shape=jax.ShapeDtypeStruct((s, nh * HEAD_DIM), BF16),
        compiler_params=_params(2, 32),
        name="dilated",
    )(pb, pb, pb, mult)


def _gelu_tanh(x):
    return 0.5 * x * (1.0 + jnp.tanh(0.7978845608028654 * (x + 0.044715 * x * x * x)))


def _compress_kernel(x_ref, pe_ref, w1_ref, w2_ref, o_ref):
    half = CMP_LEN // 2
    n_chunk = x_ref.shape[1]
    first = jnp.zeros((n_chunk, CMP_HIDDEN), F32)
    second = jnp.zeros((n_chunk, CMP_HIDDEN), F32)
    for l in range(half):
        xl = x_ref[l].astype(F32)
        xa = (xl + pe_ref[l:l + 1, :]).astype(BF16)
        xb = (xl + pe_ref[half + l:half + l + 1, :]).astype(BF16)
        first += _dot(xa, w1_ref[l * HEAD_DIM:(l + 1) * HEAD_DIM, :].astype(BF16))
        second += _dot(xb, w1_ref[(half + l) * HEAD_DIM:(half + l + 1) * HEAD_DIM, :].astype(BF16))
    pre = first + pltpu.roll(second, n_chunk - 1, axis=0)
    o_ref[...] = _dot(_gelu_tanh(pre).astype(BF16), w2_ref[...].astype(BF16)).astype(o_ref.dtype)


def compress(xt, which, pe, w1, w2, j):
    _, ng, half, n_chunk, dh = xt.shape
    return pl.pallas_call(
        _compress_kernel,
        grid=(ng,),
        in_specs=[pl.BlockSpec((None, None, half, n_chunk, dh), lambda g: (which, g, 0, 0, 0)),
                  pl.BlockSpec((None, CMP_LEN, dh), lambda g: (j, 0, 0)),
                  pl.BlockSpec((None, CMP_LEN * dh, CMP_HIDDEN), lambda g: (j, 0, 0)),
                  pl.BlockSpec((None, CMP_HIDDEN, dh), lambda g: (j, 0, 0))],
        out_specs=pl.BlockSpec((None, n_chunk, dh), lambda g: (g, 0, 0)),
        out_shape=jax.ShapeDtypeStruct((ng, n_chunk, dh), BF16),
        compiler_params=_params(1, 40),
        name="compress",
    )(xt, pe, w1, w2)


def _cmp_select_kernel(q_ref, ck_ref, cv_ref, ov_ref, ocmp_ref, sel_ref, *, scale):
    i = pl.program_id(0)
    n_cmp = ck_ref.shape[1]
    n_sel = ov_ref.shape[1]
    t_c = i * BLK + lax.broadcasted_iota(jnp.int32, (BLK, n_cmp), 0)
    n_idx = lax.broadcasted_iota(jnp.int32, (BLK, n_cmp), 1)
    cmask = n_idx * CMP_STRIDE + (CMP_LEN - 1) <= t_c
    t_s = i * BLK + lax.broadcasted_iota(jnp.int32, (BLK, n_sel), 0)
    m_idx = lax.broadcasted_iota(jnp.int32, (BLK, n_sel), 1)
    cur = jnp.right_shift(t_s, 6)
    valid = m_idx * SEL_LEN <= t_s
    forced = (m_idx == 0) | (m_idx == cur) | (m_idx == cur - 1)
    bonus = jnp.where(forced, FORCE_BONUS, 0.0)
    m_f = m_idx.astype(F32)
    ov = ov_ref[...]

    for g in range(NSA_KV_HEADS):
        ck = ck_ref[g]
        cv = cv_ref[g]
        psum = jnp.zeros((BLK, n_cmp), F32)
        for jh in range(NSA_HPG):
            hd = g * NSA_HPG + jh
            s = _dot_nt(q_ref[:, hd * HEAD_DIM:(hd + 1) * HEAD_DIM], ck) * scale
            sm = jnp.where(cmask, s, NEG)
            m = jnp.max(sm, axis=-1, keepdims=True)
            m = jnp.where(m > 0.5 * NEG, m, 0.0)
            p = jnp.exp(sm - m)
            p = p / jnp.maximum(jnp.sum(p, axis=-1, keepdims=True), 1e-30)
            ocmp_ref[:, hd * HEAD_DIM:(hd + 1) * HEAD_DIM] = _dot(p.astype(BF16), cv)
            psum = psum + p
        hi = psum.astype(BF16)
        lo = (psum - hi.astype(F32)).astype(BF16)
        imp = _dot(hi, ov) + _dot(lo, ov)
        score = jnp.where(valid, imp + bonus, NEG)
        sel = jnp.zeros((BLK, n_sel), F32)
        for _ in range(SEL_TOPK):
            mx = jnp.max(score, axis=-1, keepdims=True)
            first = jnp.min(jnp.where(score == mx, m_f, float(n_sel)), axis=-1, keepdims=True)
            hit = m_f == first
            sel = jnp.where(hit & (mx > 0.5 * NEG), 1.0, sel)
            score = jnp.where(hit, NEG, score)
        sel_ref[g] = sel.astype(sel_ref.dtype)


def _overlap_table(n_chunk, n_sel):
    starts = jnp.arange(n_chunk) * CMP_STRIDE
    sel_start = jnp.arange(n_sel) * SEL_LEN
    return ((starts[:, None] <= sel_start[None] + SEL_LEN - 1)
            & (starts[:, None] + CMP_LEN - 1 >= sel_start[None])).astype(BF16)


def cmp_select(pc, ck, cv):
    s = pc.shape[0]
    n_chunk = ck.shape[1]
    n_sel = s // SEL_LEN
    ov = _overlap_table(n_chunk, n_sel)
    qw = NSA_HEADS * HEAD_DIM
    full = lambda a: pl.BlockSpec(a.shape, lambda i: (0,) * a.ndim)
    return pl.pallas_call(
        functools.partial(_cmp_select_kernel, scale=HEAD_DIM ** -0.5),
        grid=(s // BLK,),
        in_specs=[pl.BlockSpec((BLK, qw), lambda i: (i, 0)), full(ck), full(cv), full(ov)],
        out_specs=[pl.BlockSpec((BLK, qw), lambda i: (i, 0)),
                   pl.BlockSpec((NSA_KV_HEADS, BLK, n_sel), lambda i: (0, i, 0))],
        out_shape=[jax.ShapeDtypeStruct((s, qw), F32),
                   jax.ShapeDtypeStruct((NSA_KV_HEADS, s, n_sel), BF16)],
        compiler_params=_params(1, 32),
        name="cmp_select",
    )(pc, ck, cv, ov)


def _nsa_attn_kernel(q_ref, ks_ref, vs_ref, kw_ref, vw_ref, sel_ref, gate_ref, ocmp_ref,
                     o_ref, m_scr, l_scr, acc_scr, *, scale):
    i = pl.program_id(1)
    hp = NSA_HPG
    q4 = jnp.concatenate([q_ref[:, j * HEAD_DIM:(j + 1) * HEAD_DIM] for j in range(hp)], axis=0)
    sel = sel_ref[...]
    n_sel = sel.shape[1]
    rows = lax.broadcasted_iota(jnp.int32, (BLK, BLK), 0)
    lanes = lax.broadcasted_iota(jnp.int32, (BLK, BLK), 1)
    sel_rows = lax.broadcasted_iota(jnp.int32, (n_sel, BLK), 0)
    sel_half = (lax.broadcasted_iota(jnp.int32, (n_sel, BLK), 1) >= SEL_LEN).astype(jnp.int32)
    t_q = i * BLK + rows

    m_scr[...] = jnp.full_like(m_scr, NEG)
    l_scr[...] = jnp.zeros_like(l_scr)
    acc_scr[...] = jnp.zeros_like(acc_scr)

    def body(kb, carry):
        off = pl.multiple_of(kb * BLK, BLK)
        k = ks_ref[pl.ds(off, BLK), :]
        v = vs_ref[pl.ds(off, BLK), :]
        s = (_dot_nt(q4, k) * scale).reshape(hp, BLK, BLK)
        expand = (sel_rows == 2 * kb + sel_half).astype(BF16)
        picked = _dot(sel, expand)
        ok = (picked > 0.5) & (kb * BLK + lanes <= t_q)
        sm = s + jnp.where(ok, 0.0, NEG)[None]
        m_old = m_scr[...]
        m_new = jnp.maximum(m_old, jnp.max(sm, axis=-1, keepdims=True))
        alpha = jnp.exp(m_old - m_new)
        p = jnp.exp(sm - m_new)
        l_scr[...] = alpha * l_scr[...] + jnp.sum(p, axis=-1, keepdims=True)
        pv = _dot(p.reshape(hp * BLK, BLK).astype(BF16), v).reshape(hp, BLK, HEAD_DIM)
        acc_scr[...] = alpha * acc_scr[...] + pv
        m_scr[...] = m_new
        return carry

    lax.fori_loop(0, i + 1, body, 0)
    o_slc = acc_scr[...] / l_scr[...]

    back = jnp.minimum(i, WIN_LEN // BLK)
    start = pl.multiple_of((i - back) * BLK, BLK)
    kwin = kw_ref[pl.ds(start, WIN_KEYS), :]
    vwin = vw_ref[pl.ds(start, WIN_KEYS), :]
    s = (_dot_nt(q4, kwin) * scale).reshape(hp, BLK, WIN_KEYS)
    dist = (back * BLK + lax.broadcasted_iota(jnp.int32, (BLK, WIN_KEYS), 0)
            - lax.broadcasted_iota(jnp.int32, (BLK, WIN_KEYS), 1))
    sm = jnp.where(((dist >= 0) & (dist < WIN_LEN))[None], s, NEG)
    m = jnp.max(sm, axis=-1, keepdims=True)
    p = jnp.exp(sm - m)
    p = p / jnp.sum(p, axis=-1, keepdims=True)
    o_win = _dot(p.reshape(hp * BLK, WIN_KEYS).astype(BF16), vwin).reshape(hp, BLK, HEAD_DIM)

    gs = _sigmoid(gate_ref[...])
    for j in range(hp):
        o = (gs[:, 3 * j:3 * j + 1] * ocmp_ref[:, j * HEAD_DIM:(j + 1) * HEAD_DIM]
             + gs[:, 3 * j + 1:3 * j + 2] * o_slc[j]
             + gs[:, 3 * j + 2:3 * j + 3] * o_win[j])
        o_ref[:, j * HEAD_DIM:(j + 1) * HEAD_DIM] = o.astype(o_ref.dtype)


def nsa_attn(pc, sel, gates, ocmp):
    s = pc.shape[0]
    assert s >= WIN_KEYS
    hp, dh = NSA_HPG, HEAD_DIM
    gw = hp * dh
    kv0 = NSA_HEADS
    ng = NSA_KV_HEADS
    kv = lambda seg: pl.BlockSpec((s, dh), lambda g, i: (0, kv0 + seg * ng + g))
    return pl.pallas_call(
        functools.partial(_nsa_attn_kernel, scale=dh ** -0.5),
        grid=(ng, s // BLK),
        in_specs=[pl.BlockSpec((BLK, gw), lambda g, i: (i, g)),
                  kv(2), kv(3), kv(4), kv(5),
                  pl.BlockSpec((None, BLK, sel.shape[2]), lambda g, i: (g, i, 0)),
                  pl.BlockSpec((None, BLK, 3 * hp), lambda g, i: (g, i, 0)),
                  pl.BlockSpec((BLK, gw), lambda g, i: (i, g))],
        out_specs=pl.BlockSpec((BLK, gw), lambda g, i: (i, g)),
        out_shape=jax.ShapeDtypeStruct((s, NSA_HEADS * dh), BF16),
        scratch_shapes=[pltpu.VMEM((hp, BLK, 1), F32), pltpu.VMEM((hp, BLK, 1), F32),
                        pltpu.VMEM((hp, BLK, dh), F32)],
        compiler_params=_params(2, 40),
        name="nsa_attn",
    )(pc, pc, pc, pc, pc, sel, gates, ocmp)


def _rope_tables(positions):
    pos = positions.astype(F32)[:, None]
    half = RET_DK // 2
    inv = RET_ROPE_BASE ** (-jnp.arange(half, dtype=F32) / half)
    ang = pos * inv
    cos, sin = jnp.cos(ang), jnp.sin(ang)
    full = (jnp.concatenate([cos, cos], axis=1), jnp.concatenate([-sin, sin], axis=1))

    half = ROPE_DIM // 2
    inv = ROPE_THETA ** (-jnp.arange(half, dtype=F32) / half)
    ang = pos * inv
    cos, sin = jnp.cos(ang), jnp.sin(ang)
    rest = HEAD_DIM - ROPE_DIM
    ones = jnp.ones((pos.shape[0], rest), F32)
    zeros = jnp.zeros((pos.shape[0], rest), F32)
    zh = jnp.zeros_like(sin)
    part = (jnp.concatenate([cos, cos, ones], axis=1),
            jnp.concatenate([zh, sin, zeros], axis=1),
            jnp.concatenate([-sin, zh, zeros], axis=1))
    return full, part


def _ffn(h, x, w1, w3, w2b, norm_g, layer, which, gi_post, next_layer, gi_next):
    act_main = ffn_up(h, w1, w3, layer, which, 0, FF_MAIN, 768)
    act_tail = ffn_up(h, w1, w3, layer, which, FF_MAIN // FF_TAIL, FF_TAIL, FF_TAIL)
    return down(act_main, w2b, (layer, which), 768, x, norm_g, layer, gi_post,
                next_layer, gi_next, 0.5, tail=(act_tail, FF_MAIN // FF_TAIL))


def _ab_mixer(h, x, w_in, w_out_b, j, rope_full, rope_part, norm_g, layer):
    pa = proj(h, w_in, j, 0, AB_A_COLS, 1024, rope="full", rope_tiles=(0, 1), tabs=rope_full)
    pb = proj(h, w_in, j, AB_A_COLS // 1024, AB_B_COLS, 1024, rope="part",
              rope_tiles=(0, 1), tabs=rope_part)
    cat = jnp.concatenate([retention(pa), dilated(pb)], axis=1)
    return down(cat, w_out_b, (j,), 768, x, norm_g, layer, 3, layer, 4, 1.0)


def _nsa_mixer(h, x, w_in, w_gate, w_out_b, ck_w1, ck_w2, cv_w1, cv_w2, pe_k, pe_v, j,
               rope_part, norm_g, layer):
    s = h.shape[0]
    pc = proj(h, w_in, j, 0, C_MAIN_COLS, 512, rope="part", rope_tiles=(0, 1, 2, 3, 4, 6, 8),
              tabs=rope_part)
    gates = proj(h, w_gate, j, 0, LANES, LANES, out_dtype=F32)
    ng, dh, half = NSA_KV_HEADS, HEAD_DIM, CMP_LEN // 2
    qw = NSA_HEADS * dh
    xt = pc[:, qw:qw + 2 * ng * dh].reshape(s // half, half, 2, ng, dh).transpose(2, 3, 1, 0, 4)
    ck = compress(xt, 0, pe_k, ck_w1, ck_w2, j)
    cv = compress(xt, 1, pe_v, cv_w1, cv_w2, j)
    ocmp, sel = cmp_select(pc, ck, cv)
    g12 = gates[:, :C_GATES].reshape(s, ng, 3 * NSA_HPG).transpose(1, 0, 2)
    o = nsa_attn(pc, sel, g12, ocmp)
    return down(o, w_out_b, (j,), 1024, x, norm_g, layer, 3, layer, 4, 1.0)


def kernel(x, positions, norm_g, ffn_w1, ffn_w3, ffn_w2, ab_w_in, ab_w_out, c_w_in, c_w_out,
           c_ck_w1, c_ck_w2, c_cv_w1, c_cv_w2, c_pe_k, c_pe_v):
    b, s, d = x.shape
    assert b == 1 and d == D_MODEL
    xs = x[0]
    rope_full, rope_part = _rope_tables(positions[0])
    w2b = ffn_w2.astype(BF16)
    ab_out_b = ab_w_out.astype(BF16)
    c_out_b = c_w_out.astype(BF16)
    w_gate = jnp.pad(c_w_in[:, :, C_MAIN_COLS:], ((0, 0), (0, 0), (0, LANES - C_GATES)))

    h = prenorm(xs, norm_g, 0, 0)
    for layer in range(DEPTH):
        j = layer // 2
        xs, h = _ffn(h, xs, ffn_w1, ffn_w3, w2b, norm_g, layer, 0, 1, layer, 2)
        if layer % 2 == 0:
            xs, h = _ab_mixer(h, xs, ab_w_in, ab_out_b, j, rope_full, rope_part, norm_g, layer)
        else:
            xs, h = _nsa_mixer(h, xs, c_w_in, w_gate, c_out_b, c_ck_w1, c_ck_w2, c_cv_w1,
                               c_cv_w2, c_pe_k, c_pe_v, j, rope_part, norm_g, layer)
        nxt = min(layer + 1, DEPTH - 1)
        xs, h = _ffn(h, xs, ffn_w1, ffn_w3, w2b, norm_g, layer, 1, 5, nxt, 0)
    return xs[None]
```
